```python
import math
import jax, jax.numpy as jnp
from jax import lax
import numpy as np

D_MODEL = 4096
BATCH = 4
SEQ = 4096
DEPTH = 4

CHUNK = 64
N_MIXERS = 2
CONV_WIDTH = 3
D_FF = 11008
N_HEADS = 32
Q_LORA_RANK = 1024
KV_LORA_RANK = 512
QK_NOPE_DIM = 128
QK_ROPE_DIM = 64
QK_HEAD_DIM = QK_NOPE_DIM + QK_ROPE_DIM
V_HEAD_DIM = 128
Q_BLOCK = 128
ROPE_BASE = 10000.0
NORM_EPS = 1e-6
ATTN_SCALE = QK_HEAD_DIM ** -0.5

kernel_name = "hybrid_shortconv_mla_convffn_trunk"


def rms_norm(x, g):
    xf = x.astype(jnp.float32)
    y = xf * lax.rsqrt(jnp.mean(xf * xf, axis=-1, keepdims=True) + NORM_EPS)
    return (y * g.astype(jnp.float32)).astype(x.dtype)


def causal_dwconv(x, w):
    s = x.shape[1]
    xp = jnp.pad(x, ((0, 0), (CONV_WIDTH - 1, 0), (0, 0)))
    y = w[0] * xp[:, 0:s]
    for k in range(1, CONV_WIDTH):
        y = y + w[k] * xp[:, k:k + s]
    return y


def rope_cos_sin(positions):
    inv_freq = ROPE_BASE ** (-jnp.arange(0, QK_ROPE_DIM, 2, dtype=jnp.float32) / QK_ROPE_DIM)
    ang = positions.astype(jnp.float32)[..., None] * inv_freq
    return jnp.cos(ang), jnp.sin(ang)


def apply_rope(x, cos, sin):
    half = x.shape[-1] // 2
    x1 = x[..., :half].astype(jnp.float32)
    x2 = x[..., half:].astype(jnp.float32)
    out = jnp.concatenate([x1 * cos - x2 * sin, x1 * sin + x2 * cos], axis=-1)
    return out.astype(x.dtype)


def short_conv_mixer(h, w_in, conv_w, w_out):
    bcv = h @ w_in
    b, c, v = jnp.split(bcv, 3, axis=-1)
    y = b * causal_dwconv(c * v, conv_w)
    return y @ w_out


def mla_mixer(h, cos, sin, wq_a, q_norm_g, wq_b, wkv_a, kv_norm_g, wkv_b, wo):
    bsz, s, _ = h.shape
    q = (rms_norm(h @ wq_a, q_norm_g) @ wq_b).reshape(bsz, s, N_HEADS, QK_HEAD_DIM)
    q_nope = q[..., :QK_NOPE_DIM]
    q_rope = apply_rope(q[..., QK_NOPE_DIM:], cos[:, :, None], sin[:, :, None])
    kv = h @ wkv_a
    c_kv = rms_norm(kv[..., :KV_LORA_RANK], kv_norm_g)
    k_rope = apply_rope(kv[..., KV_LORA_RANK:], cos, sin)
    wkv_b_r = wkv_b.reshape(KV_LORA_RANK, N_HEADS, QK_NOPE_DIM + V_HEAD_DIM)
    w_uk = wkv_b_r[..., :QK_NOPE_DIM]
    w_uv = wkv_b_r[..., QK_NOPE_DIM:]
    n_blocks = s // Q_BLOCK
    key_chunk = jnp.arange(s) // CHUNK

    def to_blocks(t):
        return jnp.moveaxis(t.reshape(bsz, n_blocks, Q_BLOCK, *t.shape[2:]), 1, 0)

    def attend(args):
        qn, qr, blk = args
        q_lat = jnp.einsum('bqhd,chd->bqhc', qn, w_uk)
        scores = (jnp.einsum('bqhc,bkc->bhqk', q_lat, c_kv, preferred_element_type=jnp.float32)
                  + jnp.einsum('bqhr,bkr->bhqk', qr, k_rope, preferred_element_type=jnp.float32)) * ATTN_SCALE
        q_chunk = (blk * Q_BLOCK + jnp.arange(Q_BLOCK)) // CHUNK
        mask = key_chunk[None, :] <= q_chunk[:, None]
        scores = jnp.where(mask[None, None], scores, -jnp.inf)
        p = jax.nn.softmax(scores, axis=-1).astype(c_kv.dtype)
        o_lat = jnp.einsum('bhqk,bkc->bqhc', p, c_kv)
        return jnp.einsum('bqhc,chd->bqhd', o_lat, w_uv)

    o = lax.map(attend, (to_blocks(q_nope), to_blocks(q_rope), jnp.arange(n_blocks)))
    o = jnp.moveaxis(o, 0, 1).reshape(bsz, s, N_HEADS * V_HEAD_DIM)
    return o @ wo


def conv_ffn(h, w_gate, w_up, conv_w, conv_b, w_down):
    g = causal_dwconv(h @ w_gate, conv_w) + conv_b
    return (jax.nn.silu(g) * (h @ w_up)) @ w_down


def setup_inputs(seed: int = 0) -> dict:
    key = jax.random.key(seed)
    ks = jax.random.split(key, 24)
    n_conv = (DEPTH + N_MIXERS - 1) // N_MIXERS
    n_mla = DEPTH // N_MIXERS
    f32 = jnp.float32

    def w(k, shape, fan_in):
        return jax.random.normal(k, shape, f32) * (fan_in ** -0.5)

    def gain(k, shape):
        return 1.0 + 0.01 * jax.random.normal(k, shape, f32)

    x = jax.random.normal(ks[0], (BATCH, SEQ, D_MODEL), f32)
    start = jax.random.randint(ks[1], (BATCH, 1), 0, 64) * CHUNK
    positions = (start + jnp.arange(SEQ, dtype=jnp.int32)[None, :]).astype(jnp.int32)
    return {
        "x": x,
        "positions": positions,
        "attn_norm_g": gain(ks[2], (DEPTH, D_MODEL)),
        "ffn_norm_g": gain(ks[3], (DEPTH, D_MODEL)),
        "final_norm_g": gain(ks[4], (D_MODEL,)),
        "sc_w_in": w(ks[5], (n_conv, D_MODEL, 3 * D_MODEL), D_MODEL),
        "sc_conv_w": w(ks[6], (n_conv, CONV_WIDTH, D_MODEL), CONV_WIDTH),
        "sc_w_out": w(ks[7], (n_conv, D_MODEL, D_MODEL), D_MODEL),
        "mla_wq_a": w(ks[8], (n_mla, D_MODEL, Q_LORA_RANK), D_MODEL),
        "mla_q_norm_g": gain(ks[9], (n_mla, Q_LORA_RANK)),
        "mla_wq_b": w(ks[10], (n_mla, Q_LORA_RANK, N_HEADS * QK_HEAD_DIM), Q_LORA_RANK),
        "mla_wkv_a": w(ks[11], (n_mla, D_MODEL, KV_LORA_RANK + QK_ROPE_DIM), D_MODEL),
        "mla_kv_norm_g": gain(ks[12], (n_mla, KV_LORA_RANK)),
        "mla_wkv_b": w(ks[13], (n_mla, KV_LORA_RANK, N_HEADS * (QK_NOPE_DIM + V_HEAD_DIM)), KV_LORA_RANK),
        "mla_wo": w(ks[14], (n_mla, N_HEADS * V_HEAD_DIM, D_MODEL), N_HEADS * V_HEAD_DIM),
        "ffn_w_gate": w(ks[15], (DEPTH, D_MODEL, D_FF), D_MODEL),
        "ffn_w_up": w(ks[16], (DEPTH, D_MODEL, D_FF), D_MODEL),
        "ffn_conv_w": w(ks[17], (DEPTH, CONV_WIDTH, D_FF), CONV_WIDTH),
        "ffn_conv_b": 0.01 * jax.random.normal(ks[18], (DEPTH, D_FF), f32),
        "ffn_w_down": w(ks[19], (DEPTH, D_FF, D_MODEL), D_FF),
    }


def reference(x, positions, attn_norm_g, ffn_norm_g, final_norm_g,
              sc_w_in, sc_conv_w, sc_w_out,
              mla_wq_a, mla_q_norm_g, mla_wq_b, mla_wkv_a, mla_kv_norm_g, mla_wkv_b, mla_wo,
              ffn_w_gate, ffn_w_up, ffn_conv_w, ffn_conv_b, ffn_w_down):
    cos, sin = rope_cos_sin(positions)
    h = x
    for i in range(DEPTH):
        j = i // N_MIXERS
        hn = rms_norm(h, attn_norm_g[i])
        if i % N_MIXERS == 0:
            mix = short_conv_mixer(hn, sc_w_in[j], sc_conv_w[j], sc_w_out[j])
        else:
            mix = mla_mixer(hn, cos, sin, mla_wq_a[j], mla_q_norm_g[j], mla_wq_b[j],
                            mla_wkv_a[j], mla_kv_norm_g[j], mla_wkv_b[j], mla_wo[j])
        h = h + mix
        h = h + conv_ffn(rms_norm(h, ffn_norm_g[i]), ffn_w_gate[i], ffn_w_up[i],
                         ffn_conv_w[i], ffn_conv_b[i], ffn_w_down[i])
    return rms_norm(h, final_norm_g)
```

```python
import functools
import math

import jax
import jax.numpy as jnp
from jax import lax
from jax.experimental import pallas as pl
from jax.experimental.pallas import tpu as pltpu

CHUNK = 64
CONV_WIDTH = 3
QK_NOPE_DIM = 128
QK_ROPE_DIM = 64
V_HEAD_DIM = 128
ROPE_BASE = 10000.0
NORM_EPS = 1e-6
ATTN_SCALE = (QK_NOPE_DIM + QK_ROPE_DIM) ** -0.5

V7X_VMEM_BYTES = 64 * 1024 * 1024
LANES = 128
SUBLANES = 8

_F32 = jnp.float32
_BF16 = jnp.bfloat16
_MASK_VALUE = -0.7 * float(jnp.finfo(jnp.float32).max)
_HALO = SUBLANES


def _tile(dim, want):
    t = min(dim, want)
    while dim % t:
        t //= 2
    return t


def _nbytes(shape, dtype):
    return math.prod(shape) * jnp.dtype(dtype).itemsize


def _params(semantics, *buffers):
    need = sum(_nbytes(s, d) * n for s, d, n in buffers)
    limit = min(need + (4 << 20), V7X_VMEM_BYTES - (4 << 20))
    return pltpu.CompilerParams(dimension_semantics=semantics, vmem_limit_bytes=int(limit))


def _rmsnorm_kernel(x_ref, g_ref, o_ref):
    x = x_ref[...]
    y = x * lax.rsqrt(jnp.mean(x * x, axis=-1, keepdims=True) + NORM_EPS)
    o_ref[...] = (y * g_ref[...]).astype(o_ref.dtype)


def _rmsnorm(x, g, out_dtype):
    m, d = x.shape
    tm = _tile(m, 256)
    return pl.pallas_call(
        _rmsnorm_kernel,
        grid=(m // tm,),
        in_specs=[pl.BlockSpec((tm, d), lambda i: (i, 0)), pl.BlockSpec((1, d), lambda i: (0, 0))],
        out_specs=pl.BlockSpec((tm, d), lambda i: (i, 0)),
        out_shape=jax.ShapeDtypeStruct((m, d), out_dtype),
        compiler_params=_params(("arbitrary",), ((tm, d), _F32, 4), ((tm, d), out_dtype, 2)),
        name="rmsnorm",
    )(x, g.reshape(1, d))


def _causal_conv(ext_ref, vals, cw, first_tile):
    tm = vals.shape[0]

    @pl.when(first_tile)
    def _():
        ext_ref[0:_HALO, :] = jnp.zeros((_HALO, vals.shape[1]), _F32)

    ext_ref[_HALO:, :] = vals
    y = cw[0:1, :] * ext_ref[pl.ds(_HALO - 2, tm), :]
    y = y + cw[1:2, :] * ext_ref[pl.ds(_HALO - 1, tm), :]
    y = y + cw[2:3, :] * vals
    ext_ref[0:_HALO, :] = ext_ref[pl.ds(tm, _HALO), :]
    return y


def _sc_in_kernel(x_ref, wb_ref, wc_ref, wv_ref, cw_ref, o_ref, ext_ref, *, tiles_per_seq):
    x = x_ref[...]
    c = jnp.dot(x, wc_ref[...], preferred_element_type=_F32)
    v = jnp.dot(x, wv_ref[...], preferred_element_type=_F32)
    conv = _causal_conv(ext_ref, c * v, cw_ref[...], pl.program_id(1) % tiles_per_seq == 0)
    b = jnp.dot(x, wb_ref[...], preferred_element_type=_F32)
    o_ref[...] = (b * conv).astype(o_ref.dtype)


def _sc_in(xn, w_in, conv_w, seq):
    m, k = xn.shape
    d = w_in.shape[1] // 3
    tm, tn = _tile(seq, 1024), _tile(d, 512)
    nb = d // tn
    kern = functools.partial(_sc_in_kernel, tiles_per_seq=seq // tm)
    w_spec = lambda off: pl.BlockSpec((k, tn), lambda j, i: (0, off * nb + j))
    return pl.pallas_call(
        kern,
        grid=(nb, m // tm),
        in_specs=[
            pl.BlockSpec((tm, k), lambda j, i: (i, 0)),
            w_spec(0), w_spec(1), w_spec(2),
            pl.BlockSpec((CONV_WIDTH, tn), lambda j, i: (0, j)),
        ],
        out_specs=pl.BlockSpec((tm, tn), lambda j, i: (i, j)),
        out_shape=jax.ShapeDtypeStruct((m, d), _BF16),
        scratch_shapes=[pltpu.VMEM((tm + _HALO, tn), _F32)],
        compiler_params=_params(
            ("arbitrary", "arbitrary"),
            ((tm, k), _BF16, 2), ((k, tn), _BF16, 6), ((tm, tn), _BF16, 2), ((tm + _HALO, tn), _F32, 6)),
        name="sc_in",
    )(xn, w_in, w_in, w_in, conv_w)


def _ffn_in_kernel(x_ref, wg_ref, wu_ref, cw_ref, cb_ref, o_ref, ext_ref, *, tiles_per_seq):
    x = x_ref[...]
    gate = jnp.dot(x, wg_ref[...], preferred_element_type=_F32)
    g = _causal_conv(ext_ref, gate, cw_ref[...], pl.program_id(1) % tiles_per_seq == 0) + cb_ref[...]
    u = jnp.dot(x, wu_ref[...], preferred_element_type=_F32)
    act = g * (1.0 / (1.0 + jnp.exp(-g)))
    o_ref[...] = (act * u).astype(o_ref.dtype)


def _ffn_in(xn, w_gate, w_up, conv_w, conv_b, seq):
    m, k = xn.shape
    f = w_gate.shape[1]
    tm, tn = _tile(seq, 1024), _tile(f, 512)
    kern = functools.partial(_ffn_in_kernel, tiles_per_seq=seq // tm)
    w_spec = pl.BlockSpec((k, tn), lambda j, i: (0, j))
    return pl.pallas_call(
        kern,
        grid=(f // tn, m // tm),
        in_specs=[
            pl.BlockSpec((tm, k), lambda j, i: (i, 0)),
            w_spec, w_spec,
            pl.BlockSpec((CONV_WIDTH, tn), lambda j, i: (0, j)),
            pl.BlockSpec((1, tn), lambda j, i: (0, j)),
        ],
        out_specs=pl.BlockSpec((tm, tn), lambda j, i: (i, j)),
        out_shape=jax.ShapeDtypeStruct((m, f), _BF16),
        scratch_shapes=[pltpu.VMEM((tm + _HALO, tn), _F32)],
        compiler_params=_params(
            ("arbitrary", "arbitrary"),
            ((tm, k), _BF16, 2), ((k, tn), _BF16, 4), ((tm, tn), _BF16, 2), ((tm + _HALO, tn), _F32, 6)),
        name="ffn_in",
    )(xn, w_gate, w_up, conv_w, conv_b)


def _mm_res_kernel(x_ref, w_ref, h_ref, o_ref, acc_ref, *, nk):
    part = jnp.dot(x_ref[...], w_ref[...], preferred_element_type=_F32)
    if nk == 1:
        o_ref[...] = h_ref[...] + part
        return
    k = pl.program_id(2)

    @pl.when(k == 0)
    def _():
        acc_ref[...] = part

    @pl.when(jnp.logical_and(k > 0, k < nk - 1))
    def _():
        acc_ref[...] += part

    @pl.when(k == nk - 1)
    def _():
        o_ref[...] = h_ref[...] + (acc_ref[...] + part)


def _mm_res(x, w, h, *, tm_want, tn_want, tk_want):
    m, k = x.shape
    n = w.shape[1]
    tm, tn, tk = _tile(m, tm_want), _tile(n, tn_want), _tile(k, tk_want)
    nk = k // tk
    kern = functools.partial(_mm_res_kernel, nk=nk)
    acc_shape = (tm, tn) if nk > 1 else (SUBLANES, LANES)
    return pl.pallas_call(
        kern,
        grid=(n // tn, m // tm, nk),
        in_specs=[
            pl.BlockSpec((tm, tk), lambda j, i, kk: (i, kk)),
            pl.BlockSpec((tk, tn), lambda j, i, kk: (kk, j)),
            pl.BlockSpec((tm, tn), lambda j, i, kk: (i, j)),
        ],
        out_specs=pl.BlockSpec((tm, tn), lambda j, i, kk: (i, j)),
        out_shape=jax.ShapeDtypeStruct((m, n), _F32),
        scratch_shapes=[pltpu.VMEM(acc_shape, _F32)],
        compiler_params=_params(
            ("arbitrary", "arbitrary", "arbitrary"),
            ((tm, tk), _BF16, 2), ((tk, tn), _BF16, 2), ((tm, tn), _F32, 6), (acc_shape, _F32, 1)),
        name="mm_res",
    )(x, w, h)


def _rope_table_kernel(pos_ref, invf_ref, sign_ref, cos_ref, sin_ref):
    ang = pos_ref[...].astype(_F32) * invf_ref[...]
    cos_ref[...] = jnp.cos(ang)
    sin_ref[...] = jnp.sin(ang) * sign_ref[...]


def _rope_tables(positions):
    m = positions.size
    half = QK_ROPE_DIM // 2
    inv_freq = ROPE_BASE ** (-jnp.arange(0, QK_ROPE_DIM, 2, dtype=_F32) / QK_ROPE_DIM)
    reps = LANES // half
    invf = jnp.tile(inv_freq, reps).reshape(1, LANES)
    sign = jnp.tile(jnp.concatenate([-jnp.ones((half,), _F32), jnp.ones((half,), _F32)]), reps // 2).reshape(1, LANES)
    tm = _tile(m, 1024)
    row = pl.BlockSpec((1, LANES), lambda i: (0, 0))
    out = pl.BlockSpec((tm, LANES), lambda i: (i, 0))
    return pl.pallas_call(
        _rope_table_kernel,
        grid=(m // tm,),
        in_specs=[pl.BlockSpec((tm, 1), lambda i: (i, 0)), row, row],
        out_specs=[out, out],
        out_shape=[jax.ShapeDtypeStruct((m, LANES), _F32)] * 2,
        compiler_params=_params(("arbitrary",), ((tm, LANES), _F32, 8)),
        name="rope_tables",
    )(positions.reshape(m, 1), invf, sign)


def _rms_rows(y, g):
    return y * lax.rsqrt(jnp.mean(y * y, axis=-1, keepdims=True) + NORM_EPS) * g


def _mla_in_kernel(x_ref, w_ref, qg_ref, kg_ref, cos_ref, sin_ref, qa_ref, ckv_ref, kr_ref, *, q_rank, kv_rank):
    x = x_ref[...]
    qa = jnp.dot(x, w_ref[:, :q_rank], preferred_element_type=_F32)
    qa_ref[...] = _rms_rows(qa, qg_ref[...]).astype(qa_ref.dtype)
    ckv = jnp.dot(x, w_ref[:, q_rank:q_rank + kv_rank], preferred_element_type=_F32)
    ckv_ref[...] = _rms_rows(ckv, kg_ref[...]).astype(ckv_ref.dtype)
    o = q_rank + kv_rank
    r = jnp.dot(x, w_ref[:, o:o + LANES], preferred_element_type=_F32)
    rh = jnp.dot(x, w_ref[:, o + LANES:o + 2 * LANES], preferred_element_type=_F32)
    kr_ref[...] = (r * cos_ref[...] + rh * sin_ref[...]).astype(kr_ref.dtype)


def _mla_in(xn, w1, q_norm_g, kv_norm_g, cos_t, sin_t):
    m, k = xn.shape
    q_rank, kv_rank = q_norm_g.shape[0], kv_norm_g.shape[0]
    n = w1.shape[1]
    tm = _tile(m, 512)
    kern = functools.partial(_mla_in_kernel, q_rank=q_rank, kv_rank=kv_rank)
    const = lambda shape: pl.BlockSpec(shape, lambda i: (0, 0))
    rows = lambda width: pl.BlockSpec((tm, width), lambda i: (i, 0))
    return pl.pallas_call(
        kern,
        grid=(m // tm,),
        in_specs=[rows(k), const((k, n)), const((1, q_rank)), const((1, kv_rank)), rows(LANES), rows(LANES)],
        out_specs=[rows(q_rank), rows(kv_rank), rows(LANES)],
        out_shape=[
            jax.ShapeDtypeStruct((m, q_rank), _BF16),
            jax.ShapeDtypeStruct((m, kv_rank), _BF16),
            jax.ShapeDtypeStruct((m, LANES), _BF16),
        ],
        compiler_params=_params(
            ("arbitrary",), ((tm, k), _BF16, 2), ((k, n), _BF16, 2), ((tm, n), _F32, 3), ((tm, n), _BF16, 2)),
        name="mla_in",
    )(xn, w1, q_norm_g.reshape(1, q_rank), kv_norm_g.reshape(1, kv_rank), cos_t, sin_t)


_PROJ_CHUNK = 512


def _q_up_kernel(x_ref, wn_ref, wr_ref, wrh_ref, cos_ref, sin_ref, qn_ref, qr_ref):
    x = x_ref[...]
    for c in range(0, qn_ref.shape[1], _PROJ_CHUNK):
        sl = slice(c, min(c + _PROJ_CHUNK, qn_ref.shape[1]))
        y = jnp.dot(x, wn_ref[:, sl], preferred_element_type=_F32)
        qn_ref[:, sl] = (y * ATTN_SCALE).astype(qn_ref.dtype)
    cos, sin = cos_ref[...], sin_ref[...]
    for c in range(0, qr_ref.shape[1], _PROJ_CHUNK):
        sl = slice(c, min(c + _PROJ_CHUNK, qr_ref.shape[1]))
        r = jnp.dot(x, wr_ref[:, sl], preferred_element_type=_F32)
        rh = jnp.dot(x, wrh_ref[:, sl], preferred_element_type=_F32)
        for l in range(0, sl.stop - sl.start, LANES):
            rot = r[:, l:l + LANES] * cos + rh[:, l:l + LANES] * sin
            qr_ref[:, c + l:c + l + LANES] = (rot * ATTN_SCALE).astype(qr_ref.dtype)


def _q_up(qa, wn, wr, wrh, cos_t, sin_t):
    m, k = qa.shape
    nn, nr = wn.shape[1], wr.shape[1]
    tm = _tile(m, 512)
    const = lambda shape: pl.BlockSpec(shape, lambda i: (0, 0))
    rows = lambda width: pl.BlockSpec((tm, width), lambda i: (i, 0))
    return pl.pallas_call(
        _q_up_kernel,
        grid=(m // tm,),
        in_specs=[rows(k), const((k, nn)), const((k, nr)), const((k, nr)), rows(LANES), rows(LANES)],
        out_specs=[rows(nn), rows(nr)],
        out_shape=[jax.ShapeDtypeStruct((m, nn), _BF16), jax.ShapeDtypeStruct((m, nr), _BF16)],
        compiler_params=_params(
            ("arbitrary",), ((tm, k), _BF16, 2), ((k, nn + 2 * nr), _BF16, 2), ((tm, nn + nr), _BF16, 2),
            ((tm, _PROJ_CHUNK), _F32, 6)),
        name="q_up",
    )(qa, wn, wr, wrh, cos_t, sin_t)


def _kv_up_kernel(x_ref, w_ref, o_ref):
    x = x_ref[...]
    for c in range(0, o_ref.shape[1], _PROJ_CHUNK):
        sl = slice(c, min(c + _PROJ_CHUNK, o_ref.shape[1]))
        o_ref[:, sl] = jnp.dot(x, w_ref[:, sl], preferred_element_type=_F32).astype(o_ref.dtype)


def _kv_up(ckv, w):
    m, k = ckv.shape
    n = w.shape[1]
    tm = _tile(m, 512)
    return pl.pallas_call(
        _kv_up_kernel,
        grid=(m // tm,),
        in_specs=[pl.BlockSpec((tm, k), lambda i: (i, 0)), pl.BlockSpec((k, n), lambda i: (0, 0))],
        out_specs=pl.BlockSpec((tm, n), lambda i: (i, 0)),
        out_shape=jax.ShapeDtypeStruct((m, n), _BF16),
        compiler_params=_params(
            ("arbitrary",), ((tm, k), _BF16, 2), ((k, n), _BF16, 2), ((tm, n), _BF16, 2), ((tm, _PROJ_CHUNK), _F32, 4)),
        name="kv_up",
    )(ckv, w)


_HEADS_PER_STEP = 2


def _attn_kernel(qn_ref, qr_ref, kn_ref, kr_ref, v_ref, o_ref, m_ref, l_ref, acc_ref, *, tq):
    i = pl.program_id(2)
    lane_half = lax.broadcasted_iota(jnp.int32, (tq, LANES), 1) // QK_ROPE_DIM
    qr = qr_ref[...].astype(_F32)
    q_cat = []
    for hh in range(_HEADS_PER_STEP):
        qr_h = jnp.where(lane_half == hh, qr, 0.0).astype(_BF16)
        q_cat.append(jnp.concatenate([qn_ref[:, hh * LANES:(hh + 1) * LANES], qr_h], axis=1))
    m_ref[...] = jnp.full(m_ref.shape, _MASK_VALUE, _F32)
    l_ref[...] = jnp.zeros(l_ref.shape, _F32)
    acc_ref[...] = jnp.zeros(acc_ref.shape, _F32)

    def block(j, diagonal):
        ks = pl.multiple_of(j * tq, tq)
        kr = kr_ref[pl.ds(ks, tq), :]
        for hh in range(_HEADS_PER_STEP):
            hs = slice(hh * LANES, (hh + 1) * LANES)
            k_cat = jnp.concatenate([kn_ref[pl.ds(ks, tq), hs], kr], axis=1)
            s = lax.dot_general(q_cat[hh], k_cat, (((1,), (1,)), ((), ())), preferred_element_type=_F32)
            if diagonal:
                row_chunk = lax.broadcasted_iota(jnp.int32, (tq, tq), 0) // CHUNK
                col_chunk = lax.broadcasted_iota(jnp.int32, (tq, tq), 1) // CHUNK
                s = jnp.where(col_chunk <= row_chunk, s, _MASK_VALUE)
            m_prev, l_prev = m_ref[hh], l_ref[hh]
            m_next = jnp.maximum(m_prev, jnp.max(s, axis=1, keepdims=True))
            p = jnp.exp(s - pltpu.repeat(m_next, tq // LANES, 1))
            alpha = jnp.exp(m_prev - m_next)
            l_ref[hh] = alpha * l_prev + jnp.sum(p, axis=1, keepdims=True)
            m_ref[hh] = m_next
            pv = jnp.dot(p.astype(_BF16), v_ref[pl.ds(ks, tq), hs], preferred_element_type=_F32)
            acc_ref[hh] = acc_ref[hh] * alpha + pv

    def full_block(j, carry):
        block(j, False)
        return carry

    lax.fori_loop(0, i, full_block, 0)
    block(i, True)
    for hh in range(_HEADS_PER_STEP):
        o_ref[:, hh * LANES:(hh + 1) * LANES] = (acc_ref[hh] / l_ref[hh]).astype(o_ref.dtype)


def _attention(qn, qr, kvb, kr, batch, seq, n_heads):
    m = batch * seq
    tq = _tile(seq, 512)
    nq = seq // tq
    n_pairs = n_heads // _HEADS_PER_STEP
    pair_w = _HEADS_PER_STEP * LANES
    kern = functools.partial(_attn_kernel, tq=tq)
    stat = pltpu.VMEM((_HEADS_PER_STEP, tq, LANES), _F32)
    return pl.pallas_call(
        kern,
        grid=(batch, n_pairs, nq),
        in_specs=[
            pl.BlockSpec((tq, pair_w), lambda b, p, i: (b * nq + i, p)),
            pl.BlockSpec((tq, LANES), lambda b, p, i: (b * nq + i, p)),
            pl.BlockSpec((seq, pair_w), lambda b, p, i: (b, p)),
            pl.BlockSpec((seq, LANES), lambda b, p, i: (b, 0)),
            pl.BlockSpec((seq, pair_w), lambda b, p, i: (b, n_pairs + p)),
        ],
        out_specs=pl.BlockSpec((tq, pair_w), lambda b, p, i: (b * nq + i, p)),
        out_shape=jax.ShapeDtypeStruct((m, n_heads * V_HEAD_DIM), _BF16),
        scratch_shapes=[stat, stat, stat],
        compiler_params=_params(
            ("arbitrary", "arbitrary", "arbitrary"),
            ((seq, 2 * pair_w + LANES), _BF16, 2), ((tq, 2 * pair_w + LANES), _BF16, 2),
            ((_HEADS_PER_STEP, tq, LANES), _F32, 3), ((tq, tq), _F32, 8)),
        name="attention",
    )(qn, qr, kvb, kr, kvb)


def _prep_mla_weights(wq_a, wq_b, wkv_a, wkv_b, n_heads, kv_rank):
    half = QK_ROPE_DIM // 2
    k_rope = wkv_a[:, kv_rank:]
    k_rope_rh = jnp.concatenate([k_rope[:, half:], k_rope[:, :half]], axis=1)
    w1 = jnp.concatenate([wq_a, wkv_a[:, :kv_rank], k_rope, k_rope, k_rope_rh, k_rope_rh], axis=1).astype(_BF16)

    q_rank = wq_b.shape[0]
    wq = wq_b.reshape(q_rank, n_heads, QK_NOPE_DIM + QK_ROPE_DIM)
    wn = wq[:, :, :QK_NOPE_DIM].reshape(q_rank, n_heads * QK_NOPE_DIM).astype(_BF16)
    rope = wq[:, :, QK_NOPE_DIM:]
    wr = rope.reshape(q_rank, n_heads * QK_ROPE_DIM).astype(_BF16)
    wrh = jnp.concatenate([rope[:, :, half:], rope[:, :, :half]], axis=2)
    wrh = wrh.reshape(q_rank, n_heads * QK_ROPE_DIM).astype(_BF16)

    wkv = wkv_b.reshape(kv_rank, n_heads, QK_NOPE_DIM + V_HEAD_DIM)
    w_uk = wkv[:, :, :QK_NOPE_DIM].reshape(kv_rank, n_heads * QK_NOPE_DIM)
    w_uv = wkv[:, :, QK_NOPE_DIM:].reshape(kv_rank, n_heads * V_HEAD_DIM)
    w_kv = jnp.concatenate([w_uk, w_uv], axis=1).astype(_BF16)
    return w1, wn, wr, wrh, w_kv


def _pad_to(x, axis, mult):
    pad = (-x.shape[axis]) % mult
    if pad == 0:
        return x
    widths = [(0, 0)] * x.ndim
    widths[axis] = (0, pad)
    return jnp.pad(x, widths)


_FFN_PAD = 512
_FFN_DOWN_TK = 2816


def kernel(x, positions, attn_norm_g, ffn_norm_g, final_norm_g, sc_w_in, sc_conv_w, sc_w_out, mla_wq_a, mla_q_norm_g, mla_wq_b, mla_wkv_a, mla_kv_norm_g, mla_wkv_b, mla_wo, ffn_w_gate, ffn_w_up, ffn_conv_w, ffn_conv_b, ffn_w_down):
    batch, seq, d_model = x.shape
    depth = attn_norm_g.shape[0]
    n_heads = mla_wo.shape[1] // V_HEAD_DIM
    kv_rank = mla_kv_norm_g.shape[1]
    assert n_heads % _HEADS_PER_STEP == 0 and seq % CHUNK == 0

    h = x.reshape(batch * seq, d_model)
    cos_t, sin_t = _rope_tables(positions)

    for i in range(depth):
        j = i // 2
        hn = _rmsnorm(h, attn_norm_g[i], _BF16)
        if i % 2 == 0:
            y = _sc_in(hn, sc_w_in[j].astype(_BF16), sc_conv_w[j], seq)
            h = _mm_res(y, sc_w_out[j].astype(_BF16), h, tm_want=512, tn_want=1024, tk_want=1 << 30)
        else:
            w1, wn, wr, wrh, w_kv = _prep_mla_weights(
                mla_wq_a[j], mla_wq_b[j], mla_wkv_a[j], mla_wkv_b[j], n_heads, kv_rank)
            qa, ckv, kr = _mla_in(hn, w1, mla_q_norm_g[j], mla_kv_norm_g[j], cos_t, sin_t)
            qn, qr = _q_up(qa, wn, wr, wrh, cos_t, sin_t)
            kvb = _kv_up(ckv, w_kv)
            o = _attention(qn, qr, kvb, kr, batch, seq, n_heads)
            h = _mm_res(o, mla_wo[j].astype(_BF16), h, tm_want=512, tn_want=1024, tk_want=1 << 30)

        hn = _rmsnorm(h, ffn_norm_g[i], _BF16)
        w_gate = _pad_to(ffn_w_gate[i].astype(_BF16), 1, _FFN_PAD)
        w_up = _pad_to(ffn_w_up[i].astype(_BF16), 1, _FFN_PAD)
        w_down = _pad_to(ffn_w_down[i].astype(_BF16), 0, _FFN_PAD)
        conv_w = _pad_to(ffn_conv_w[i], 1, _FFN_PAD)
        conv_b = _pad_to(ffn_conv_b[i].reshape(1, -1), 1, _FFN_PAD)
        a = _ffn_in(hn, w_gate, w_up, conv_w, conv_b, seq)
        h = _mm_res(a, w_down, h, tm_want=1024, tn_want=1024, tk_want=_FFN_DOWN_TK)

    out = _rmsnorm(h, final_norm_g, x.dtype)
    return out.reshape(batch, seq, d_model)
```

```python
import functools
import math

import jax
import jax.numpy as jnp
from jax import lax
from jax.experimental import pallas as pl
from jax.experimental.pallas import tpu as pltpu

CHUNK = 64
CONV_WIDTH = 3
QK_NOPE_DIM = 128
QK_ROPE_DIM = 64
V_HEAD_DIM = 128
ROPE_BASE = 10000.0
NORM_EPS = 1e-6
ATTN_SCALE = (QK_NOPE_DIM + QK_ROPE_DIM) ** -0.5
_Q_SCALE = ATTN_SCALE * math.log2(math.e)

V7X_VMEM_BYTES = 64 * 1024 * 1024
LANES = 128
SUBLANES = 8

_F32 = jnp.float32
_BF16 = jnp.bfloat16
_MASK_VALUE = -0.7 * float(jnp.finfo(jnp.float32).max)
_HALO = SUBLANES


def _tile(dim, want):
    t = min(dim, want)
    while dim % t:
        t //= 2
    return t


def _nbytes(shape, dtype):
    return math.prod(shape) * jnp.dtype(dtype).itemsize


def _params(semantics, *buffers):
    need = sum(_nbytes(s, d) * n for s, d, n in buffers)
    limit = min(need + (4 << 20), V7X_VMEM_BYTES - (4 << 20))
    return pltpu.CompilerParams(dimension_semantics=semantics, vmem_limit_bytes=int(limit))


def _rmsnorm_kernel(x_ref, g_ref, o_ref):
    x = x_ref[...]
    y = x * lax.rsqrt(jnp.mean(x * x, axis=-1, keepdims=True) + NORM_EPS)
    o_ref[...] = (y * g_ref[...]).astype(o_ref.dtype)


def _rmsnorm(x, g, out_dtype):
    m, d = x.shape
    tm = _tile(m, 256)
    return pl.pallas_call(
        _rmsnorm_kernel,
        grid=(m // tm,),
        in_specs=[pl.BlockSpec((tm, d), lambda i: (i, 0)), pl.BlockSpec((1, d), lambda i: (0, 0))],
        out_specs=pl.BlockSpec((tm, d), lambda i: (i, 0)),
        out_shape=jax.ShapeDtypeStruct((m, d), out_dtype),
        compiler_params=_params(("arbitrary",), ((tm, d), _F32, 4), ((tm, d), out_dtype, 2)),
        name="rmsnorm",
    )(x, g.reshape(1, d))


def _conv_begin(ext_ref, first_tile):
    @pl.when(first_tile)
    def _():
        ext_ref[0:_HALO, :] = jnp.zeros((_HALO, ext_ref.shape[1]), _F32)


def _conv_rows(ext_ref, vals, cw, r):
    cm = vals.shape[0]
    ext_ref[_HALO + r:_HALO + r + cm, :] = vals
    y = cw[0:1, :] * ext_ref[pl.ds(_HALO - 2 + r, cm), :]
    y = y + cw[1:2, :] * ext_ref[pl.ds(_HALO - 1 + r, cm), :]
    return y + cw[2:3, :] * vals


def _conv_end(ext_ref):
    tm = ext_ref.shape[0] - _HALO
    ext_ref[0:_HALO, :] = ext_ref[pl.ds(tm, _HALO), :]


_ROW_CHUNK = 256


def _sc_in_kernel(x_ref, wb_ref, wc_ref, wv_ref, cw_ref, o_ref, ext_ref, *, tiles_per_seq):
    _conv_begin(ext_ref, pl.program_id(1) % tiles_per_seq == 0)
    cw = cw_ref[...]
    tm = x_ref.shape[0]
    cm = min(tm, _ROW_CHUNK)
    for r in range(0, tm, cm):
        x = x_ref[r:r + cm, :]
        c = jnp.dot(x, wc_ref[...], preferred_element_type=_F32)
        v = jnp.dot(x, wv_ref[...], preferred_element_type=_F32)
        conv = _conv_rows(ext_ref, c * v, cw, r)
        b = jnp.dot(x, wb_ref[...], preferred_element_type=_F32)
        o_ref[r:r + cm, :] = (b * conv).astype(o_ref.dtype)
    _conv_end(ext_ref)


def _sc_in(xn, w_in, conv_w, layer, seq):
    m, k = xn.shape
    d = w_in.shape[2] // 3
    tm, tn = _tile(seq, 1024), _tile(d, 512)
    nb = d // tn
    kern = functools.partial(_sc_in_kernel, tiles_per_seq=seq // tm)
    w_spec = lambda off: pl.BlockSpec((None, k, tn), lambda j, i: (layer, 0, off * nb + j))
    return pl.pallas_call(
        kern,
        grid=(nb, m // tm),
        in_specs=[
            pl.BlockSpec((tm, k), lambda j, i: (i, 0)),
            w_spec(0), w_spec(1), w_spec(2),
            pl.BlockSpec((None, CONV_WIDTH, tn), lambda j, i: (layer, 0, j)),
        ],
        out_specs=pl.BlockSpec((tm, tn), lambda j, i: (i, j)),
        out_shape=jax.ShapeDtypeStruct((m, d), _BF16),
        scratch_shapes=[pltpu.VMEM((tm + _HALO, tn), _F32)],
        compiler_params=_params(
            ("arbitrary", "arbitrary"),
            ((tm, k), _BF16, 2), ((k, tn), _BF16, 6), ((tm, tn), _BF16, 2), ((tm + _HALO, tn), _F32, 6)),
        name="sc_in",
    )(xn, w_in, w_in, w_in, conv_w)


def _ffn_in_kernel(x_ref, wg_ref, wu_ref, cw_ref, cb_ref, o_ref, ext_ref, *, tiles_per_seq):
    _conv_begin(ext_ref, pl.program_id(1) % tiles_per_seq == 0)
    cw, cb = cw_ref[...], cb_ref[...]
    tm = x_ref.shape[0]
    cm = min(tm, _ROW_CHUNK)
    for r in range(0, tm, cm):
        x = x_ref[r:r + cm, :]
        gate = jnp.dot(x, wg_ref[...], preferred_element_type=_F32)
        g = _conv_rows(ext_ref, gate, cw, r) + cb
        u = jnp.dot(x, wu_ref[...], preferred_element_type=_F32)
        act = g * (1.0 / (1.0 + jnp.exp(-g)))
        o_ref[r:r + cm, :] = (act * u).astype(o_ref.dtype)
    _conv_end(ext_ref)


def _ffn_in(xn, w_gate, w_up, conv_w, conv_b, layer, seq):
    m, k = xn.shape
    f = w_gate.shape[2]
    tm, tn = _tile(seq, 1024), _tile(f, 512)
    kern = functools.partial(_ffn_in_kernel, tiles_per_seq=seq // tm)
    w_spec = pl.BlockSpec((None, k, tn), lambda j, i: (layer, 0, j))
    return pl.pallas_call(
        kern,
        grid=(f // tn, m // tm),
        in_specs=[
            pl.BlockSpec((tm, k), lambda j, i: (i, 0)),
            w_spec, w_spec,
            pl.BlockSpec((None, CONV_WIDTH, tn), lambda j, i: (layer, 0, j)),
            pl.BlockSpec((None, 1, tn), lambda j, i: (layer, 0, j)),
        ],
        out_specs=pl.BlockSpec((tm, tn), lambda j, i: (i, j)),
        out_shape=jax.ShapeDtypeStruct((m, f), _BF16),
        scratch_shapes=[pltpu.VMEM((tm + _HALO, tn), _F32)],
        compiler_params=_params(
            ("arbitrary", "arbitrary"),
            ((tm, k), _BF16, 2), ((k, tn), _BF16, 4), ((tm, tn), _BF16, 2), ((tm + _HALO, tn), _F32, 6)),
        name="ffn_in",
    )(xn, w_gate, w_up, conv_w, conv_b)


def _mm_res_kernel(x_ref, w_ref, h_ref, o_ref, *, nk):
    if nk > 1:
        @pl.when(pl.program_id(2) == 0)
        def _():
            o_ref[...] = h_ref[...]

    tm = x_ref.shape[0]
    cm = min(tm, _ROW_CHUNK)
    for r in range(0, tm, cm):
        part = jnp.dot(x_ref[r:r + cm, :], w_ref[...], preferred_element_type=_F32)
        base = h_ref if nk == 1 else o_ref
        o_ref[r:r + cm, :] = base[r:r + cm, :] + part


def _mm_res(x, w, layer, h, *, tm_want, tn_want, tk_want):
    m, k = x.shape
    n = w.shape[2]
    tm, tn, tk = _tile(m, tm_want), _tile(n, tn_want), _tile(k, tk_want)
    nk = k // tk
    kern = functools.partial(_mm_res_kernel, nk=nk)
    return pl.pallas_call(
        kern,
        grid=(n // tn, m // tm, nk),
        in_specs=[
            pl.BlockSpec((tm, tk), lambda j, i, kk: (i, kk)),
            pl.BlockSpec((None, tk, tn), lambda j, i, kk: (layer, kk, j)),
            pl.BlockSpec((tm, tn), lambda j, i, kk: (i, j)),
        ],
        out_specs=pl.BlockSpec((tm, tn), lambda j, i, kk: (i, j)),
        out_shape=jax.ShapeDtypeStruct((m, n), _F32),
        compiler_params=_params(
            ("arbitrary", "arbitrary", "arbitrary"),
            ((tm, tk), _BF16, 2), ((tk, tn), _BF16, 2), ((tm, tn), _F32, 4), ((_ROW_CHUNK, tn), _F32, 4)),
        name="mm_res",
    )(x, w, h)


def _rope_table_kernel(pos_ref, invf_ref, sign_ref, cos_ref, sin_ref):
    ang = pos_ref[...].astype(_F32) * invf_ref[...]
    cos_ref[...] = jnp.cos(ang)
    sin_ref[...] = jnp.sin(ang) * sign_ref[...]


def _rope_tables(positions):
    m = positions.size
    half = QK_ROPE_DIM // 2
    inv_freq = ROPE_BASE ** (-jnp.arange(0, QK_ROPE_DIM, 2, dtype=_F32) / QK_ROPE_DIM)
    reps = LANES // half
    invf = jnp.tile(inv_freq, reps).reshape(1, LANES)
    sign = jnp.tile(jnp.concatenate([-jnp.ones((half,), _F32), jnp.ones((half,), _F32)]), reps // 2).reshape(1, LANES)
    tm = _tile(m, 1024)
    row = pl.BlockSpec((1, LANES), lambda i: (0, 0))
    out = pl.BlockSpec((tm, LANES), lambda i: (i, 0))
    return pl.pallas_call(
        _rope_table_kernel,
        grid=(m // tm,),
        in_specs=[pl.BlockSpec((tm, 1), lambda i: (i, 0)), row, row],
        out_specs=[out, out],
        out_shape=[jax.ShapeDtypeStruct((m, LANES), _F32)] * 2,
        compiler_params=_params(("arbitrary",), ((tm, LANES), _F32, 8)),
        name="rope_tables",
    )(positions.reshape(m, 1), invf, sign)


def _rms_rows(y, g):
    return y * lax.rsqrt(jnp.mean(y * y, axis=-1, keepdims=True) + NORM_EPS) * g


def _mla_in_kernel(x_ref, w_ref, qg_ref, kg_ref, cos_ref, sin_ref, qa_ref, ckv_ref, kr_ref, *, q_rank, kv_rank):
    x = x_ref[...]
    qa = jnp.dot(x, w_ref[:, :q_rank], preferred_element_type=_F32)
    qa_ref[...] = _rms_rows(qa, qg_ref[...]).astype(qa_ref.dtype)
    ckv = jnp.dot(x, w_ref[:, q_rank:q_rank + kv_rank], preferred_element_type=_F32)
    ckv_ref[...] = _rms_rows(ckv, kg_ref[...]).astype(ckv_ref.dtype)
    o = q_rank + kv_rank
    r = jnp.dot(x, w_ref[:, o:o + LANES], preferred_element_type=_F32)
    rh = jnp.dot(x, w_ref[:, o + LANES:o + 2 * LANES], preferred_element_type=_F32)
    kr_ref[...] = (r * cos_ref[...] + rh * sin_ref[...]).astype(kr_ref.dtype)


def _mla_in(xn, w1, q_norm_g, kv_norm_g, cos_t, sin_t):
    m, k = xn.shape
    q_rank, kv_rank = q_norm_g.shape[0], kv_norm_g.shape[0]
    n = w1.shape[1]
    tm = _tile(m, 512)
    kern = functools.partial(_mla_in_kernel, q_rank=q_rank, kv_rank=kv_rank)
    const = lambda shape: pl.BlockSpec(shape, lambda i: (0, 0))
    rows = lambda width: pl.BlockSpec((tm, width), lambda i: (i, 0))
    return pl.pallas_call(
        kern,
        grid=(m // tm,),
        in_specs=[rows(k), const((k, n)), const((1, q_rank)), const((1, kv_rank)), rows(LANES), rows(LANES)],
        out_specs=[rows(q_rank), rows(kv_rank), rows(LANES)],
        out_shape=[
            jax.ShapeDtypeStruct((m, q_rank), _BF16),
            jax.ShapeDtypeStruct((m, kv_rank), _BF16),
            jax.ShapeDtypeStruct((m, LANES), _BF16),
        ],
        compiler_params=_params(
            ("arbitrary",), ((tm, k), _BF16, 2), ((k, n), _BF16, 2), ((tm, n), _F32, 3), ((tm, n), _BF16, 2)),
        name="mla_in",
    )(xn, w1, q_norm_g.reshape(1, q_rank), kv_norm_g.reshape(1, kv_rank), cos_t, sin_t)


_PROJ_CHUNK = 512


def _q_up_kernel(x_ref, wn_ref, wr_ref, wrh_ref, cos_ref, sin_ref, qn_ref, qr_ref):
    x = x_ref[...]
    for c in range(0, qn_ref.shape[1], _PROJ_CHUNK):
        sl = slice(c, min(c + _PROJ_CHUNK, qn_ref.shape[1]))
        y = jnp.dot(x, wn_ref[:, sl], preferred_element_type=_F32)
        qn_ref[:, sl] = (y * _Q_SCALE).astype(qn_ref.dtype)
    cos, sin = cos_ref[...], sin_ref[...]
    for c in range(0, qr_ref.shape[1], _PROJ_CHUNK):
        sl = slice(c, min(c + _PROJ_CHUNK, qr_ref.shape[1]))
        r = jnp.dot(x, wr_ref[:, sl], preferred_element_type=_F32)
        rh = jnp.dot(x, wrh_ref[:, sl], preferred_element_type=_F32)
        for l in range(0, sl.stop - sl.start, LANES):
            rot = r[:, l:l + LANES] * cos + rh[:, l:l + LANES] * sin
            qr_ref[:, c + l:c + l + LANES] = (rot * _Q_SCALE).astype(qr_ref.dtype)


def _q_up(qa, wn, wr, wrh, cos_t, sin_t):
    m, k = qa.shape
    nn, nr = wn.shape[1], wr.shape[1]
    tm = _tile(m, 512)
    const = lambda shape: pl.BlockSpec(shape, lambda i: (0, 0))
    rows = lambda width: pl.BlockSpec((tm, width), lambda i: (i, 0))
    return pl.pallas_call(
        _q_up_kernel,
        grid=(m // tm,),
        in_specs=[rows(k), const((k, nn)), const((k, nr)), const((k, nr)), rows(LANES), rows(LANES)],
        out_specs=[rows(nn), rows(nr)],
        out_shape=[jax.ShapeDtypeStruct((m, nn), _BF16), jax.ShapeDtypeStruct((m, nr), _BF16)],
        compiler_params=_params(
            ("arbitrary",), ((tm, k), _BF16, 2), ((k, nn + 2 * nr), _BF16, 2), ((tm, nn + nr), _BF16, 2),
            ((tm, _PROJ_CHUNK), _F32, 6)),
        name="q_up",
    )(qa, wn, wr, wrh, cos_t, sin_t)


_NT_DIMS = (((1,), (1,)), ((), ()))


def _kv_up_kernel(x_ref, wk_ref, wvt_ref, kn_ref, vt_ref):
    x = x_ref[...]
    for c in range(0, kn_ref.shape[1], _PROJ_CHUNK):
        sl = slice(c, min(c + _PROJ_CHUNK, kn_ref.shape[1]))
        kn_ref[:, sl] = jnp.dot(x, wk_ref[:, sl], preferred_element_type=_F32).astype(kn_ref.dtype)
    for c in range(0, vt_ref.shape[0], _PROJ_CHUNK):
        sl = slice(c, min(c + _PROJ_CHUNK, vt_ref.shape[0]))
        vt = lax.dot_general(wvt_ref[sl, :], x, _NT_DIMS, preferred_element_type=_F32)
        vt_ref[sl, :] = vt.astype(vt_ref.dtype)


def _kv_up(ckv, w_uk, w_uv_t, tile):
    m, k = ckv.shape
    n = w_uk.shape[1]
    return pl.pallas_call(
        _kv_up_kernel,
        grid=(m // tile,),
        in_specs=[
            pl.BlockSpec((tile, k), lambda i: (i, 0)),
            pl.BlockSpec((k, n), lambda i: (0, 0)),
            pl.BlockSpec((n, k), lambda i: (0, 0)),
        ],
        out_specs=[pl.BlockSpec((tile, n), lambda i: (i, 0)), pl.BlockSpec((None, n, tile), lambda i: (i, 0, 0))],
        out_shape=[jax.ShapeDtypeStruct((m, n), _BF16), jax.ShapeDtypeStruct((m // tile, n, tile), _BF16)],
        compiler_params=_params(
            ("arbitrary",), ((tile, k), _BF16, 2), ((k, n), _BF16, 4), ((tile, n), _BF16, 4),
            ((tile, _PROJ_CHUNK), _F32, 4)),
        name="kv_up",
    )(ckv, w_uk, w_uv_t)


_HEADS_PER_STEP = 4
_ATTN_TILE = 512


def _attn_kernel(qn_ref, qr_ref, kn_ref, kr_ref, vt_ref, o_ref, m_ref, l_ref, acc_ref, *, tq):
    i = pl.program_id(2)
    lane_half = lax.broadcasted_iota(jnp.int32, (tq, LANES), 1) // QK_ROPE_DIM
    heads_per_lane_block = LANES // QK_ROPE_DIM
    q_cat = []
    for hh in range(_HEADS_PER_STEP):
        blk = hh // heads_per_lane_block
        qr = qr_ref[:, blk * LANES:(blk + 1) * LANES].astype(_F32)
        qr_h = jnp.where(lane_half == hh % heads_per_lane_block, qr, 0.0).astype(_BF16)
        q_cat.append(jnp.concatenate([qn_ref[:, hh * LANES:(hh + 1) * LANES], qr_h], axis=1))
    m_ref[...] = jnp.full(m_ref.shape, _MASK_VALUE, _F32)
    l_ref[...] = jnp.zeros(l_ref.shape, _F32)
    acc_ref[...] = jnp.zeros(acc_ref.shape, _F32)

    def block(j, diagonal):
        ks = pl.multiple_of(j * tq, tq)
        kr = kr_ref[pl.ds(ks, tq), :]
        scores = []
        for hh in range(_HEADS_PER_STEP):
            hs = slice(hh * LANES, (hh + 1) * LANES)
            k_cat = jnp.concatenate([kn_ref[pl.ds(ks, tq), hs], kr], axis=1)
            scores.append(lax.dot_general(k_cat, q_cat[hh], _NT_DIMS, preferred_element_type=_F32))
        for hh in range(_HEADS_PER_STEP):
            hs = slice(hh * LANES, (hh + 1) * LANES)
            st = scores[hh]
            if diagonal:
                key_chunk = lax.broadcasted_iota(jnp.int32, (tq, tq), 0) // CHUNK
                qry_chunk = lax.broadcasted_iota(jnp.int32, (tq, tq), 1) // CHUNK
                st = jnp.where(key_chunk <= qry_chunk, st, _MASK_VALUE)
            m_prev = m_ref[hh]
            m_next = jnp.maximum(m_prev, jnp.max(st, axis=0, keepdims=True))
            p = jnp.exp2(st - m_next)
            alpha = jnp.exp2(m_prev - m_next)
            l_ref[hh] = alpha * l_ref[hh] + jnp.sum(p, axis=0, keepdims=True)
            m_ref[hh] = m_next
            pv = jnp.dot(vt_ref[j, hs, :], p.astype(_BF16), preferred_element_type=_F32)
            acc_ref[hh] = acc_ref[hh] * alpha + pv

    def full_block(j, carry):
        block(j, False)
        return carry

    lax.fori_loop(0, i, full_block, 0)
    block(i, True)
    for hh in range(_HEADS_PER_STEP):
        out_t = acc_ref[hh] / l_ref[hh]
        o_ref[:, hh * LANES:(hh + 1) * LANES] = out_t.T.astype(o_ref.dtype)


def _attention(qn, qr, kn, kr, vt, batch, seq, n_heads, tq):
    m = batch * seq
    nq = seq // tq
    n_pairs = n_heads // _HEADS_PER_STEP
    pair_w = _HEADS_PER_STEP * LANES
    kern = functools.partial(_attn_kernel, tq=tq)
    stat = pltpu.VMEM((_HEADS_PER_STEP, 1, tq), _F32)
    return pl.pallas_call(
        kern,
        grid=(batch, n_pairs, nq),
        in_specs=[
            pl.BlockSpec((tq, pair_w), lambda b, p, i: (b * nq + i, p)),
            pl.BlockSpec((tq, _HEADS_PER_STEP * QK_ROPE_DIM), lambda b, p, i: (b * nq + i, p)),
            pl.BlockSpec((seq, pair_w), lambda b, p, i: (b, p)),
            pl.BlockSpec((seq, LANES), lambda b, p, i: (b, 0)),
            pl.BlockSpec((nq, pair_w, tq), lambda b, p, i: (b, p, 0)),
        ],
        out_specs=pl.BlockSpec((tq, pair_w), lambda b, p, i: (b * nq + i, p)),
        out_shape=jax.ShapeDtypeStruct((m, n_heads * V_HEAD_DIM), _BF16),
        scratch_shapes=[stat, stat, pltpu.VMEM((_HEADS_PER_STEP, V_HEAD_DIM, tq), _F32)],
        compiler_params=_params(
            ("arbitrary", "arbitrary", "arbitrary"),
            ((seq, 2 * pair_w + LANES), _BF16, 2), ((tq, 2 * pair_w + LANES), _BF16, 2),
            ((_HEADS_PER_STEP, V_HEAD_DIM + 2 * SUBLANES, tq), _F32, 1), ((tq, tq), _F32, 8)),
        name="attention",
    )(qn, qr, kn, kr, vt)


def _prep_mla_weights(wq_a, wq_b, wkv_a, wkv_b, n_heads, kv_rank):
    half = QK_ROPE_DIM // 2
    k_rope = wkv_a[:, kv_rank:]
    k_rope_rh = jnp.concatenate([k_rope[:, half:], k_rope[:, :half]], axis=1)
    w1 = jnp.concatenate([wq_a, wkv_a[:, :kv_rank], k_rope, k_rope, k_rope_rh, k_rope_rh], axis=1).astype(_BF16)

    q_rank = wq_b.shape[0]
    wq = wq_b.reshape(q_rank, n_heads, QK_NOPE_DIM + QK_ROPE_DIM)
    wn = wq[:, :, :QK_NOPE_DIM].reshape(q_rank, n_heads * QK_NOPE_DIM).astype(_BF16)
    rope = wq[:, :, QK_NOPE_DIM:]
    wr = rope.reshape(q_rank, n_heads * QK_ROPE_DIM).astype(_BF16)
    wrh = jnp.concatenate([rope[:, :, half:], rope[:, :, :half]], axis=2)
    wrh = wrh.reshape(q_rank, n_heads * QK_ROPE_DIM).astype(_BF16)

    wkv = wkv_b.reshape(kv_rank, n_heads, QK_NOPE_DIM + V_HEAD_DIM)
    w_uk = wkv[:, :, :QK_NOPE_DIM].reshape(kv_rank, n_heads * QK_NOPE_DIM).astype(_BF16)
    w_uv_t = wkv[:, :, QK_NOPE_DIM:].reshape(kv_rank, n_heads * V_HEAD_DIM).T.astype(_BF16)
    return w1, wn, wr, wrh, w_uk, w_uv_t


def _pad_to(x, axis, mult):
    pad = (-x.shape[axis]) % mult
    if pad == 0:
        return x
    widths = [(0, 0)] * x.ndim
    widths[axis] = (0, pad)
    return jnp.pad(x, widths)


_FFN_PAD = 512
_FFN_DOWN_TK = 2816


def kernel(x, positions, attn_norm_g, ffn_norm_g, final_norm_g, sc_w_in, sc_conv_w, sc_w_out, mla_wq_a, mla_q_norm_g, mla_wq_b, mla_wkv_a, mla_kv_norm_g, mla_wkv_b, mla_wo, ffn_w_gate, ffn_w_up, ffn_conv_w, ffn_conv_b, ffn_w_down):
    batch, seq, d_model = x.shape
    depth = attn_norm_g.shape[0]
    n_heads = mla_wo.shape[1] // V_HEAD_DIM
    kv_rank = mla_kv_norm_g.shape[1]
    assert n_heads % _HEADS_PER_STEP == 0 and seq % CHUNK == 0

    h = x.reshape(batch * seq, d_model)
    cos_t, sin_t = _rope_tables(positions)
    attn_tile = _tile(seq, _ATTN_TILE)

    sc_w_in_b = sc_w_in.astype(_BF16)
    sc_w_out_b = sc_w_out.astype(_BF16)
    mla_wo_b = mla_wo.astype(_BF16)
    w_gate_b = _pad_to(ffn_w_gate.astype(_BF16), 2, _FFN_PAD)
    w_up_b = _pad_to(ffn_w_up.astype(_BF16), 2, _FFN_PAD)
    w_down_b = _pad_to(ffn_w_down.astype(_BF16), 1, _FFN_PAD)
    ffn_cw = _pad_to(ffn_conv_w, 2, _FFN_PAD)
    ffn_cb = _pad_to(ffn_conv_b[:, None, :], 2, _FFN_PAD)

    for i in range(depth):
        j = i // 2
        hn = _rmsnorm(h, attn_norm_g[i], _BF16)
        if i % 2 == 0:
            y = _sc_in(hn, sc_w_in_b, sc_conv_w, j, seq)
            h = _mm_res(y, sc_w_out_b, j, h, tm_want=512, tn_want=1024, tk_want=1 << 30)
        else:
            w1, wn, wr, wrh, w_uk, w_uv_t = _prep_mla_weights(
                mla_wq_a[j], mla_wq_b[j], mla_wkv_a[j], mla_wkv_b[j], n_heads, kv_rank)
            qa, ckv, kr = _mla_in(hn, w1, mla_q_norm_g[j], mla_kv_norm_g[j], cos_t, sin_t)
            qn, qr = _q_up(qa, wn, wr, wrh, cos_t, sin_t)
            kn, vt = _kv_up(ckv, w_uk, w_uv_t, attn_tile)
            o = _attention(qn, qr, kn, kr, vt, batch, seq, n_heads, attn_tile)
            h = _mm_res(o, mla_wo_b, j, h, tm_want=512, tn_want=1024, tk_want=1 << 30)

        hn = _rmsnorm(h, ffn_norm_g[i], _BF16)
        a = _ffn_in(hn, w_gate_b, w_up_b, ffn_cw, ffn_cb, i, seq)
        h = _mm_res(a, w_down_b, i, h, tm_want=1024, tn_want=1024, tk_want=_FFN_DOWN_TK)

    out = _rmsnorm(h, final_norm_g, x.dtype)
    return out.reshape(batch, seq, d_model)
```
